```python
import math
import jax, jax.numpy as jnp
from jax import lax
import numpy as np

D_MODEL = 1024
BATCH = 8
SEQ = 2048
DEPTH = 4
DEC_BATCH = 128
DEC_SEQ = 1
PAST_LEN = 8192
PAGE_SIZE = 128

F32 = jnp.float32
N_BRANCH = 4
BRANCH_DIM = D_MODEL // 4
S5_GROUP_CH = 16
S5_GROUPS = BRANCH_DIM // S5_GROUP_CH
S5_STATE = 64
DIFF_HEADS = 4
DIFF_KV_HEADS = 2
DIFF_HEAD_DIM = BRANCH_DIM // (2 * DIFF_HEADS)
DIFF_Q_DIM = DIFF_HEADS * 2 * DIFF_HEAD_DIM
DIFF_K_DIM = DIFF_KV_HEADS * 2 * DIFF_HEAD_DIM
DIFF_V_DIM = DIFF_KV_HEADS * 2 * DIFF_HEAD_DIM
DIFF_ROW = DIFF_K_DIM + DIFF_V_DIM
ALIBI_MAX = 8.0
MLA_HEADS = 4
MLA_NOPE = 64
MLA_ROPE = 32
MLA_V = BRANCH_DIM // MLA_HEADS
MLA_Q_RANK = 256
MLA_KV_RANK = 128
MLA_ROW = MLA_KV_RANK + MLA_ROPE
ROPE_BASE = 10000.0
CONV_DIM = BRANCH_DIM
CONV_W = 3
D_FF = 2816
IN_SPLITS = (BRANCH_DIM, DIFF_Q_DIM, DIFF_K_DIM, DIFF_V_DIM, MLA_Q_RANK, MLA_KV_RANK, MLA_ROPE,
             CONV_DIM, CONV_DIM, CONV_DIM, N_BRANCH * D_MODEL)
IN_OFFSETS = tuple(int(v) for v in np.cumsum(IN_SPLITS)[:-1])
D_IN_PROJ = sum(IN_SPLITS)
Q_BLOCK = 128
LN_EPS = 1e-5
NEG_INF = -1e30

kernel_name = 'hybrid_gated_s5_diffattn_mla_shortconv_step'


def _layernorm(x, g, b):
    xf = x.astype(F32)
    mu = jnp.mean(xf, axis=-1, keepdims=True)
    xc = xf - mu
    var = jnp.mean(xc * xc, axis=-1, keepdims=True)
    return (xc * lax.rsqrt(var + LN_EPS) * g + b).astype(x.dtype)


def _rmsnorm(x, g):
    xf = x.astype(F32)
    return (xf * lax.rsqrt(jnp.mean(xf * xf, axis=-1, keepdims=True) + LN_EPS) * g).astype(x.dtype)


def _swiglu(x, w_in, w_out):
    gate, up = jnp.split(x @ w_in, 2, axis=-1)
    return (jax.nn.silu(gate) * up) @ w_out


def _rope(x, pos):
    half = x.shape[-1] // 2
    inv = ROPE_BASE ** (-jnp.arange(half, dtype=F32) / half)
    ang = pos.astype(F32)[:, None] * inv[None, :]
    shape = (1, pos.shape[0]) + (1,) * (x.ndim - 3) + (half,)
    cos = jnp.cos(ang).reshape(shape)
    sin = jnp.sin(ang).reshape(shape)
    xf = x.astype(F32)
    x1, x2 = xf[..., :half], xf[..., half:]
    return jnp.concatenate([x1 * cos - x2 * sin, x2 * cos + x1 * sin], axis=-1).astype(x.dtype)


def _map_query_blocks(fn, qs, q_pos):
    tq = q_pos.shape[0]
    blk = min(Q_BLOCK, tq)
    nb = tq // blk
    if nb == 1 or tq % blk != 0:
        return fn(qs, q_pos)
    qs_b = tuple(jnp.moveaxis(q.reshape((q.shape[0], nb, blk) + q.shape[2:]), 1, 0) for q in qs)
    out = lax.map(lambda a: fn(a[0], a[1]), (qs_b, q_pos.reshape(nb, blk)))
    out = jnp.moveaxis(out, 0, 1)
    return out.reshape((out.shape[0], tq) + out.shape[3:])


def _ssm_combine(e1, e2):
    a1r, a1i, b1r, b1i = e1
    a2r, a2i, b2r, b2i = e2
    return (a2r * a1r - a2i * a1i,
            a2r * a1i + a2i * a1r,
            a2r * b1r - a2i * b1i + b2r,
            a2r * b1i + a2i * b1r + b2i)


def _s5(u, x0_re, x0_im, l, p):
    bsz, t, _ = u.shape
    uf = u.astype(F32).reshape(bsz, t, S5_GROUPS, S5_GROUP_CH)
    lr = p['s5_lam_re'][l].astype(F32)
    li = p['s5_lam_im'][l].astype(F32)
    step = jnp.exp(p['s5_log_dt'][l].astype(F32))[:, None]
    mag = jnp.exp(lr * step)
    ar = mag * jnp.cos(li * step)
    ai = mag * jnp.sin(li * step)
    den = lr * lr + li * li
    nr, ni = ar - 1.0, ai
    kr = (nr * lr + ni * li) / den
    ki = (ni * lr - nr * li) / den
    b_re = p['s5_b_re'][l].astype(F32)
    b_im = p['s5_b_im'][l].astype(F32)
    bb_re = kr[..., None] * b_re - ki[..., None] * b_im
    bb_im = kr[..., None] * b_im + ki[..., None] * b_re
    bu_re = jnp.einsum('gpc,btgc->btgp', bb_re, uf)
    bu_im = jnp.einsum('gpc,btgc->btgp', bb_im, uf)
    a_re = jnp.broadcast_to(ar, bu_re.shape)
    a_im = jnp.broadcast_to(ai, bu_im.shape)
    A_re, A_im, S_re, S_im = lax.associative_scan(_ssm_combine, (a_re, a_im, bu_re, bu_im), axis=1)
    x0r = x0_re.astype(F32)[:, None]
    x0i = x0_im.astype(F32)[:, None]
    s_re = A_re * x0r - A_im * x0i + S_re
    s_im = A_re * x0i + A_im * x0r + S_im
    y = (jnp.einsum('gcp,btgp->btgc', p['s5_c_re'][l].astype(F32), s_re)
         - jnp.einsum('gcp,btgp->btgc', p['s5_c_im'][l].astype(F32), s_im))
    y = y.reshape(bsz, t, BRANCH_DIM) + p['s5_d'][l].astype(F32) * u.astype(F32)
    z = jax.nn.gelu(y)
    out = z * jax.nn.sigmoid(z @ p['s5_glu_w'][l].astype(F32) + p['s5_glu_b'][l].astype(F32))
    return out.astype(u.dtype), s_re[:, -1], s_im[:, -1]


def _diff_branch(q_d, kv_all, q_pos, k_pos, l, p):
    bsz, t, _ = q_d.shape
    tk = kv_all.shape[1]
    rep = DIFF_HEADS // DIFF_KV_HEADS
    q = q_d.reshape(bsz, t, DIFF_KV_HEADS, rep, 2, DIFF_HEAD_DIM)
    k = kv_all[..., :DIFF_K_DIM].reshape(bsz, tk, DIFF_KV_HEADS, 2, DIFF_HEAD_DIM)
    v = kv_all[..., DIFF_K_DIM:].reshape(bsz, tk, DIFF_KV_HEADS, 2 * DIFF_HEAD_DIM)
    lam_p = p['diff_lambda'][l].astype(F32)
    lam_init = 0.8 - 0.6 * math.exp(-0.3 * l)
    lam = jnp.exp(jnp.sum(lam_p[0] * lam_p[1])) - jnp.exp(jnp.sum(lam_p[2] * lam_p[3])) + lam_init
    slopes = (2.0 ** (-ALIBI_MAX * jnp.arange(1, DIFF_HEADS + 1, dtype=F32) / DIFF_HEADS)).reshape(DIFF_KV_HEADS, rep)
    scale = DIFF_HEAD_DIM ** -0.5

    def attend(qs, qp):
        s = jnp.einsum('bqgrcd,bkgcd->bgrcqk', qs[0], k).astype(F32) * scale
        dist = (qp[:, None] - k_pos[None, :]).astype(F32)
        s = jnp.where(dist >= 0, s - slopes[None, :, :, None, None, None] * dist, NEG_INF)
        pr = jax.nn.softmax(s, axis=-1)
        a = pr[:, :, :, 0] - lam * pr[:, :, :, 1]
        return jnp.einsum('bgrqk,bkge->bqgre', a, v)

    o = _map_query_blocks(attend, (q,), q_pos)
    o = _rmsnorm(o.astype(F32), p['diff_subln_g'][l].astype(F32)) * (1.0 - lam_init)
    return o.reshape(bsz, t, BRANCH_DIM).astype(q_d.dtype)


def _mla_branch(c_q, kv_all, q_pos, k_pos, l, p):
    bsz, t, _ = c_q.shape
    cq = _rmsnorm(c_q, p['mla_q_norm_g'][l])
    q = (cq @ p['mla_w_uq'][l]).reshape(bsz, t, MLA_HEADS, MLA_NOPE + MLA_ROPE)
    q_nope = q[..., :MLA_NOPE]
    q_rope = _rope(q[..., MLA_NOPE:], q_pos)
    q_lat = jnp.einsum('bqhn,lhn->bqhl', q_nope, p['mla_w_uk'][l])
    c_kv = kv_all[..., :MLA_KV_RANK]
    k_rope = kv_all[..., MLA_KV_RANK:]
    scale = (MLA_NOPE + MLA_ROPE) ** -0.5

    def attend(qs, qp):
        ql, qr = qs
        s = (jnp.einsum('bqhl,bkl->bhqk', ql, c_kv)
             + jnp.einsum('bqhr,bkr->bhqk', qr, k_rope)).astype(F32) * scale
        s = jnp.where(qp[:, None] >= k_pos[None, :], s, NEG_INF)
        pr = jax.nn.softmax(s, axis=-1)
        return jnp.einsum('bhqk,bkl->bqhl', pr, c_kv)

    o_lat = _map_query_blocks(attend, (q_lat, q_rope), q_pos)
    o = jnp.einsum('bqhl,lhv->bqhv', o_lat, p['mla_w_uv'][l])
    return o.reshape(bsz, t, BRANCH_DIM).astype(c_q.dtype)


def _layer(x, pos, past, l, p, alpha):
    bsz, t, _ = x.shape
    dt = x.dtype
    x = _layernorm(alpha * x + 0.5 * _swiglu(x, p['ffn_w_in'][l, 0], p['ffn_w_out'][l, 0]),
                   p['ln_g'][l, 0], p['ln_b'][l, 0])
    proj = x @ p['w_in'][l]
    (u_s5, q_d, k_d, v_d, c_q, c_kv, k_r, g_b, g_c, h_c, gate_pre) = jnp.split(proj, IN_OFFSETS, axis=-1)
    if past is None:
        x0_re = jnp.zeros((bsz, S5_GROUPS, S5_STATE), F32)
        x0_im = jnp.zeros((bsz, S5_GROUPS, S5_STATE), F32)
        buf = jnp.zeros((bsz, CONV_W - 1, CONV_DIM), dt)
    else:
        x0_re, x0_im = past['s5_re'], past['s5_im']
        buf = past['conv'].astype(dt)
    o_s5, s_re, s_im = _s5(u_s5, x0_re, x0_im, l, p)
    row_d = jnp.concatenate([k_d, v_d], axis=-1)
    kv_d = row_d if past is None else jnp.concatenate([past['diff'].astype(dt), row_d], axis=1)
    k_pos = jnp.arange(kv_d.shape[1])
    o_d = _diff_branch(q_d, kv_d, pos, k_pos, l, p)
    row_m = jnp.concatenate([_rmsnorm(c_kv, p['mla_kv_norm_g'][l]), _rope(k_r, pos)], axis=-1)
    kv_m = row_m if past is None else jnp.concatenate([past['mla'].astype(dt), row_m], axis=1)
    o_m = _mla_branch(c_q, kv_m, pos, k_pos, l, p)
    ext = jnp.concatenate([buf, g_c * h_c], axis=1)
    w = p['conv_w'][l]
    z = w[0] * ext[:, 0:t]
    for j in range(1, CONV_W):
        z = z + w[j] * ext[:, j:j + t]
    o_c = g_b * z
    new_buf = ext[:, t:]
    branches = jnp.stack([o_s5, o_d, o_m, o_c], axis=2)
    gates = jax.nn.sigmoid(gate_pre.reshape(bsz, t, N_BRANCH, D_MODEL) + p['b_gate'][l])
    merged = jnp.einsum('btnc,ncd->btnd', branches, p['w_branch'][l])
    mix = jnp.sum(gates * merged, axis=2) @ p['w_out'][l]
    x = _layernorm(alpha * x + mix, p['ln_g'][l, 1], p['ln_b'][l, 1])
    x = _layernorm(alpha * x + 0.5 * _swiglu(x, p['ffn_w_in'][l, 1], p['ffn_w_out'][l, 1]),
                   p['ln_g'][l, 2], p['ln_b'][l, 2])
    return x, (row_d, row_m, s_re, s_im, new_buf)


def setup_inputs(seed: int = 0) -> dict:
    key = jax.random.key(seed)
    ks = iter(jax.random.split(key, 40))

    def nrm(shape, s):
        return jax.random.normal(next(ks), shape, F32) * s

    n_pages = PAST_LEN // PAGE_SIZE
    n_used = DEC_BATCH * n_pages
    n_pool = n_used + max(n_used // 4, 1)
    beta = (8.0 * DEPTH) ** -0.25
    G, P, C = S5_GROUPS, S5_STATE, S5_GROUP_CH
    x_prompt = nrm((BATCH, SEQ, D_MODEL), 1.0)
    x_sample = nrm((DEC_BATCH, DEC_SEQ, D_MODEL), 1.0)
    cache_diff_kv = nrm((DEPTH, n_pool, PAGE_SIZE, DIFF_ROW), 1.0)
    cache_mla = nrm((DEPTH, n_pool, PAGE_SIZE, MLA_ROW), 1.0)
    state_s5_re = nrm((DEPTH, DEC_BATCH, G, P), 0.1)
    state_s5_im = nrm((DEPTH, DEC_BATCH, G, P), 0.1)
    state_conv = nrm((DEPTH, DEC_BATCH, CONV_W - 1, CONV_DIM), 1.0)
    page_table = jax.random.permutation(next(ks), n_pool)[:n_used].reshape(DEC_BATCH, n_pages).astype(jnp.int32)
    w_in = nrm((DEPTH, D_MODEL, D_IN_PROJ), D_MODEL ** -0.5)
    b_gate = nrm((DEPTH, N_BRANCH, D_MODEL), 0.02)
    s5_lam_re = -0.5 + nrm((DEPTH, G, P), 0.01)
    s5_lam_im = math.pi * jnp.arange(P, dtype=F32)[None, None, :] + nrm((DEPTH, G, P), 0.01)
    s5_log_dt = jax.random.uniform(next(ks), (DEPTH, G), F32, math.log(1e-3), math.log(1e-1))
    s5_b_re = nrm((DEPTH, G, P, C), (2.0 * C) ** -0.5)
    s5_b_im = nrm((DEPTH, G, P, C), (2.0 * C) ** -0.5)
    s5_c_re = nrm((DEPTH, G, C, P), P ** -0.5)
    s5_c_im = nrm((DEPTH, G, C, P), P ** -0.5)
    s5_d = nrm((DEPTH, BRANCH_DIM), 1.0)
    s5_glu_w = nrm((DEPTH, BRANCH_DIM, BRANCH_DIM), BRANCH_DIM ** -0.5)
    s5_glu_b = nrm((DEPTH, BRANCH_DIM), 0.02)
    diff_lambda = nrm((DEPTH, 4, DIFF_HEAD_DIM), 0.1)
    diff_subln_g = 1.0 + nrm((DEPTH, 2 * DIFF_HEAD_DIM), 0.02)
    mla_q_norm_g = 1.0 + nrm((DEPTH, MLA_Q_RANK), 0.02)
    mla_kv_norm_g = 1.0 + nrm((DEPTH, MLA_KV_RANK), 0.02)
    mla_w_uq = nrm((DEPTH, MLA_Q_RANK, MLA_HEADS * (MLA_NOPE + MLA_ROPE)), MLA_Q_RANK ** -0.5)
    mla_w_uk = nrm((DEPTH, MLA_KV_RANK, MLA_HEADS, MLA_NOPE), MLA_KV_RANK ** -0.5)
    mla_w_uv = nrm((DEPTH, MLA_KV_RANK, MLA_HEADS, MLA_V), MLA_KV_RANK ** -0.5)
    conv_w = nrm((DEPTH, CONV_W, CONV_DIM), CONV_W ** -0.5)
    w_branch = nrm((DEPTH, N_BRANCH, BRANCH_DIM, D_MODEL), BRANCH_DIM ** -0.5)
    w_out = nrm((DEPTH, D_MODEL, D_MODEL), beta * D_MODEL ** -0.5)
    ffn_w_in = nrm((DEPTH, 2, D_MODEL, 2 * D_FF), D_MODEL ** -0.5)
    ffn_w_out = nrm((DEPTH, 2, D_FF, D_MODEL), beta * D_FF ** -0.5)
    ln_g = 1.0 + nrm((DEPTH, 3, D_MODEL), 0.02)
    ln_b = nrm((DEPTH, 3, D_MODEL), 0.02)
    return {'x_prompt': x_prompt, 'x_sample': x_sample, 'cache_diff_kv': cache_diff_kv, 'cache_mla': cache_mla,
            'state_s5_re': state_s5_re, 'state_s5_im': state_s5_im, 'state_conv': state_conv,
            'page_table': page_table, 'w_in': w_in, 'b_gate': b_gate, 's5_lam_re': s5_lam_re,
            's5_lam_im': s5_lam_im, 's5_log_dt': s5_log_dt, 's5_b_re': s5_b_re, 's5_b_im': s5_b_im,
            's5_c_re': s5_c_re, 's5_c_im': s5_c_im, 's5_d': s5_d, 's5_glu_w': s5_glu_w, 's5_glu_b': s5_glu_b,
            'diff_lambda': diff_lambda, 'diff_subln_g': diff_subln_g, 'mla_q_norm_g': mla_q_norm_g,
            'mla_kv_norm_g': mla_kv_norm_g, 'mla_w_uq': mla_w_uq, 'mla_w_uk': mla_w_uk, 'mla_w_uv': mla_w_uv,
            'conv_w': conv_w, 'w_branch': w_branch, 'w_out': w_out, 'ffn_w_in': ffn_w_in,
            'ffn_w_out': ffn_w_out, 'ln_g': ln_g, 'ln_b': ln_b}


def reference(x_prompt, x_sample, cache_diff_kv, cache_mla, state_s5_re, state_s5_im, state_conv, page_table,
              w_in, b_gate, s5_lam_re, s5_lam_im, s5_log_dt, s5_b_re, s5_b_im, s5_c_re, s5_c_im, s5_d,
              s5_glu_w, s5_glu_b, diff_lambda, diff_subln_g, mla_q_norm_g, mla_kv_norm_g, mla_w_uq, mla_w_uk,
              mla_w_uv, conv_w, w_branch, w_out, ffn_w_in, ffn_w_out, ln_g, ln_b):
    p = dict(w_in=w_in, b_gate=b_gate, s5_lam_re=s5_lam_re, s5_lam_im=s5_lam_im, s5_log_dt=s5_log_dt,
             s5_b_re=s5_b_re, s5_b_im=s5_b_im, s5_c_re=s5_c_re, s5_c_im=s5_c_im, s5_d=s5_d,
             s5_glu_w=s5_glu_w, s5_glu_b=s5_glu_b, diff_lambda=diff_lambda, diff_subln_g=diff_subln_g,
             mla_q_norm_g=mla_q_norm_g, mla_kv_norm_g=mla_kv_norm_g, mla_w_uq=mla_w_uq, mla_w_uk=mla_w_uk,
             mla_w_uv=mla_w_uv, conv_w=conv_w, w_branch=w_branch, w_out=w_out, ffn_w_in=ffn_w_in,
             ffn_w_out=ffn_w_out, ln_g=ln_g, ln_b=ln_b)
    alpha = float((2.0 * DEPTH) ** 0.25)
    dec_b, n_pages = page_table.shape
    past_len = n_pages * cache_diff_kv.shape[2]
    pos_p = jnp.arange(x_prompt.shape[1])
    pos_s = past_len + jnp.arange(x_sample.shape[1])
    yp, ys = x_prompt, x_sample
    outs_p, outs_s = [], []
    for l in range(DEPTH):
        yp, st_p = _layer(yp, pos_p, None, l, p, alpha)
        past = {'diff': cache_diff_kv[l, page_table].reshape(dec_b, past_len, DIFF_ROW),
                'mla': cache_mla[l, page_table].reshape(dec_b, past_len, MLA_ROW),
                's5_re': state_s5_re[l], 's5_im': state_s5_im[l], 'conv': state_conv[l]}
        ys, st_s = _layer(ys, pos_s, past, l, p, alpha)
        outs_p.append(st_p)
        outs_s.append(st_s)

    def stk(outs, i, dtype):
        return jnp.stack([o[i] for o in outs], axis=0).astype(dtype)

    return (yp, ys,
            stk(outs_p, 0, cache_diff_kv.dtype), stk(outs_p, 1, cache_mla.dtype),
            stk(outs_p, 2, state_s5_re.dtype), stk(outs_p, 3, state_s5_im.dtype), stk(outs_p, 4, state_conv.dtype),
            stk(outs_s, 0, cache_diff_kv.dtype), stk(outs_s, 1, cache_mla.dtype),
            stk(outs_s, 2, state_s5_re.dtype), stk(outs_s, 3, state_s5_im.dtype), stk(outs_s, 4, state_conv.dtype))
```

```python
import functools
import math

import jax
import jax.numpy as jnp
import numpy as np
from jax import lax
from jax.experimental import pallas as pl
from jax.experimental.pallas import tpu as pltpu

F32 = jnp.float32
BF16 = jnp.bfloat16

D_MODEL = 1024
BRANCH = 256
N_BRANCH = 4
D_FF = 2816
S5_G, S5_C, S5_P = 16, 16, 64
S5_N = S5_G * S5_P
DIFF_D = 32
DIFF_ROW = 256
MLA_HEADS, MLA_NOPE, MLA_ROPE, MLA_V = 4, 64, 32, 64
MLA_Q_RANK, MLA_KV_RANK = 256, 128
MLA_ROW = MLA_KV_RANK + MLA_ROPE
MLA_PAD = 256
ROPE_BASE = 10000.0
ALIBI_MAX = 8.0
CONV_W = 3
LN_EPS = 1e-5
NEG_INF = -1e30
LANES = 128
VMEM_LIMIT = 56 * 1024 * 1024

C_U, C_QD, C_ROWD, C_CQ, C_CKV, C_KR, C_KRS, C_GB, C_GC, C_HC = (
    0, 256, 512, 768, 1024, 1152, 1280, 1408, 1664, 1920)
N_MAIN = 2176


def _params(sem):
    return pltpu.CompilerParams(dimension_semantics=sem, vmem_limit_bytes=VMEM_LIMIT)


def _const_spec(shape):
    nd = len(shape)
    return pl.BlockSpec(shape, lambda *_: (0,) * nd)


def _layernorm(y, g, b):
    mu = jnp.mean(y, axis=-1, keepdims=True)
    yc = y - mu
    var = jnp.mean(yc * yc, axis=-1, keepdims=True)
    return yc * lax.rsqrt(var + LN_EPS) * g + b


def _rmsnorm(x, g):
    return x * lax.rsqrt(jnp.mean(x * x, axis=-1, keepdims=True) + LN_EPS) * g


def _dot(a, b):
    return jnp.dot(a, b, preferred_element_type=F32)


def _dot_nt(a, b):
    return lax.dot_general(a, b, (((1,), (1,)), ((), ())), preferred_element_type=F32)


def _ffn_kernel(x_ref, wg_ref, wu_ref, wo_ref, g_ref, b_ref, o_ref, xb_ref, acc_ref, *, alpha, nj):
    j = pl.program_id(1)

    @pl.when(j == 0)
    def _():
        xb_ref[...] = x_ref[...].astype(BF16)
        acc_ref[...] = jnp.zeros_like(acc_ref)

    xb = xb_ref[...]
    hg = _dot(xb, wg_ref[...])
    hu = _dot(xb, wu_ref[...])
    act = (hg * jax.nn.sigmoid(hg)) * hu
    acc_ref[...] += _dot(act.astype(BF16), wo_ref[...])

    @pl.when(j == nj - 1)
    def _():
        y = alpha * x_ref[...] + 0.5 * acc_ref[...]
        o_ref[...] = _layernorm(y, g_ref[...], b_ref[...])


def _ffn_ln(x, w_in_bf, w_out_bf, ln_g, ln_b, l, k, lnk, alpha, tm):
    m = x.shape[0]
    tf = D_FF // 2
    nj = D_FF // tf
    return pl.pallas_call(
        functools.partial(_ffn_kernel, alpha=alpha, nj=nj),
        grid=(m // tm, nj),
        in_specs=[
            pl.BlockSpec((tm, D_MODEL), lambda i, j: (i, 0)),
            pl.BlockSpec((None, None, D_MODEL, tf), lambda i, j: (l, k, 0, j)),
            pl.BlockSpec((None, None, D_MODEL, tf), lambda i, j: (l, k, 0, j + nj)),
            pl.BlockSpec((None, None, tf, D_MODEL), lambda i, j: (l, k, j, 0)),
            pl.BlockSpec((None, None, 1, D_MODEL), lambda i, j: (l, lnk, 0, 0)),
            pl.BlockSpec((None, None, 1, D_MODEL), lambda i, j: (l, lnk, 0, 0)),
        ],
        out_specs=pl.BlockSpec((tm, D_MODEL), lambda i, j: (i, 0)),
        out_shape=jax.ShapeDtypeStruct((m, D_MODEL), F32),
        scratch_shapes=[pltpu.VMEM((tm, D_MODEL), BF16), pltpu.VMEM((tm, D_MODEL), F32)],
        compiler_params=_params(("parallel", "arbitrary")),
        name="ffn_ln",
    )(x, w_in_bf, w_in_bf, w_out_bf, ln_g, ln_b)


def _inproj_kernel(x_ref, wm_ref, place_ref, wuq_ref, wuk_ref, placer_ref, gq_ref, gkv_ref,
                   cos_ref, sin_ref,
                   u_ref, qexp_ref, rowd_ref, rowdb_ref, qcat_ref, rowm_ref, kcat_ref, gb_ref, e_ref):
    xb = x_ref[...].astype(BF16)
    proj = _dot(xb, wm_ref[...])
    cos = cos_ref[...]
    sin = sin_ref[...]

    u_ref[...] = proj[:, C_U:C_U + 256]
    qd = proj[:, C_QD:C_QD + 256].astype(BF16)
    qexp_ref[...] = _dot(qd, place_ref[...]).astype(BF16)
    rowd = proj[:, C_ROWD:C_ROWD + 256]
    rowd_ref[...] = rowd
    rowdb_ref[...] = rowd.astype(BF16)
    ckv = _rmsnorm(proj[:, C_CKV:C_CKV + 128], gkv_ref[...])
    kr = proj[:, C_KR:C_KR + 128] * cos + proj[:, C_KRS:C_KRS + 128] * sin
    rowm_ref[:, 0:MLA_KV_RANK] = ckv
    rowm_ref[:, MLA_KV_RANK:MLA_ROW] = kr[:, 0:MLA_ROPE]
    kcat_ref[:, 0:128] = ckv.astype(BF16)
    kcat_ref[:, 128:256] = kr.astype(BF16)
    cq = _rmsnorm(proj[:, C_CQ:C_CQ + 256], gq_ref[...]).astype(BF16)
    q = _dot(cq, wuq_ref[...])
    qrope = (q[:, 256:384] * cos + q[:, 384:512] * sin).astype(BF16)
    qcat = _dot(q[:, 0:256].astype(BF16), wuk_ref[...]) + _dot(qrope, placer_ref[...])
    qcat_ref[...] = qcat.astype(BF16)
    gb_ref[...] = proj[:, C_GB:C_GB + 256]
    e_ref[...] = proj[:, C_GC:C_GC + 256] * proj[:, C_HC:C_HC + 256]


def _in_proj(x, lw, cos_t, sin_t, tm):
    m = x.shape[0]
    n_tab = cos_t.shape[0] // tm
    row = lambda i: (i, 0)
    tab = lambda i: (i % n_tab, 0)
    outs = [
        ((m, 256), F32), ((m, 1024), BF16), ((m, 256), F32), ((m, 256), BF16),
        ((m, 1024), BF16), ((m, MLA_ROW), F32), ((m, MLA_PAD), BF16), ((m, 256), F32), ((m, 256), F32),
    ]
    return pl.pallas_call(
        _inproj_kernel,
        grid=(m // tm,),
        in_specs=[
            pl.BlockSpec((tm, D_MODEL), row),
            _const_spec((D_MODEL, N_MAIN)),
            _const_spec((256, 1024)),
            _const_spec((256, 512)),
            _const_spec((256, 1024)),
            _const_spec((128, 1024)),
            _const_spec((1, 256)),
            _const_spec((1, 128)),
            pl.BlockSpec((tm, LANES), tab),
            pl.BlockSpec((tm, LANES), tab),
        ],
        out_specs=[pl.BlockSpec((tm, s[1]), row) for s, _ in outs],
        out_shape=[jax.ShapeDtypeStruct(s, d) for s, d in outs],
        compiler_params=_params(("parallel",)),
        name="in_proj",
    )(x, lw["w_main"], lw["place_q"], lw["w_uq"], lw["w_uk"], lw["place_r"], lw["g_q"], lw["g_kv"],
      cos_t, sin_t)


def _s5_kernel(u_ref, x0r_ref, x0i_ref, ar_ref, ai_ref, bb_ref, cc_ref, d_ref, gw_ref, gbias_ref,
               o_ref, sr_ref, si_ref, st_ref, s_ref, *, nb, tc, nsteps):
    i = pl.program_id(0)

    @pl.when(i == 0)
    def _():
        st_ref[0] = x0r_ref[...]
        st_ref[1] = x0i_ref[...]

    u = u_ref[...]
    s_ref[...] = _dot(u.astype(BF16), bb_ref[...])
    ar = jnp.broadcast_to(ar_ref[...], (nb, S5_N))
    ai = jnp.broadcast_to(ai_ref[...], (nb, S5_N))

    def step(t, carry):
        sr, si = carry
        r0 = pl.multiple_of(t * nb, nb)
        nr = ar * sr - ai * si + s_ref[pl.ds(r0, nb), 0:S5_N]
        ni = ar * si + ai * sr + s_ref[pl.ds(r0, nb), S5_N:2 * S5_N]
        s_ref[pl.ds(r0, nb), 0:S5_N] = nr
        s_ref[pl.ds(r0, nb), S5_N:2 * S5_N] = ni
        return nr, ni

    sr, si = lax.fori_loop(0, tc, step, (st_ref[0], st_ref[1]))
    st_ref[0] = sr
    st_ref[1] = si

    y = _dot(s_ref[...].astype(BF16), cc_ref[...]) + d_ref[...] * u
    z = jax.nn.gelu(y)
    o_ref[...] = z * jax.nn.sigmoid(_dot(z.astype(BF16), gw_ref[...]) + gbias_ref[...])

    @pl.when(i == nsteps - 1)
    def _():
        sr_ref[...] = sr
        si_ref[...] = si


def _s5(u_tm, x0r, x0i, lw, nb, tc):
    rows = u_tm.shape[0]
    nsteps = rows // (nb * tc)
    rb = nb * tc
    return pl.pallas_call(
        functools.partial(_s5_kernel, nb=nb, tc=tc, nsteps=nsteps),
        grid=(nsteps,),
        in_specs=[
            pl.BlockSpec((rb, 256), lambda i: (i, 0)),
            _const_spec((nb, S5_N)), _const_spec((nb, S5_N)),
            _const_spec((1, S5_N)), _const_spec((1, S5_N)),
            _const_spec((256, 2 * S5_N)), _const_spec((2 * S5_N, 256)),
            _const_spec((1, 256)), _const_spec((256, 256)), _const_spec((1, 256)),
        ],
        out_specs=[pl.BlockSpec((rb, 256), lambda i: (i, 0)),
                   _const_spec((nb, S5_N)), _const_spec((nb, S5_N))],
        out_shape=[jax.ShapeDtypeStruct((rows, 256), F32),
                   jax.ShapeDtypeStruct((nb, S5_N), F32), jax.ShapeDtypeStruct((nb, S5_N), F32)],
        scratch_shapes=[pltpu.VMEM((2, nb, S5_N), F32), pltpu.VMEM((rb, 2 * S5_N), F32)],
        compiler_params=_params(("arbitrary",)),
        name="s5",
    )(u_tm, x0r, x0i, lw["s5_ar"], lw["s5_ai"], lw["s5_bb"], lw["s5_cc"], lw["s5_d"],
      lw["s5_gw"], lw["s5_gb"])


def _diff_slope(g, r):
    h = g * 2 + r
    return float(2.0 ** (-ALIBI_MAX * (h + 1) / 4))


def _diff_finish(x, g, sub_g, lam_init):
    lane = lax.broadcasted_iota(jnp.int32, x.shape, 1)
    sel = (lane >= 64 * g) & (lane < 64 * g + 64)
    ss = jnp.sum(jnp.where(sel, x * x, 0.0), axis=-1, keepdims=True) * (1.0 / 64)
    return x * lax.rsqrt(ss + LN_EPS) * sub_g * (1.0 - lam_init)


def _diff_prompt_kernel(q_ref, kv_ref, lam_ref, subg_ref, o_ref, m_ref, l_ref, acc_ref, *, tq, lam_init):
    qi = pl.program_id(1)
    scale = DIFF_D ** -0.5
    m_ref[...] = jnp.full_like(m_ref, NEG_INF)
    l_ref[...] = jnp.zeros_like(l_ref)
    acc_ref[...] = jnp.zeros_like(acc_ref)

    def kblock(kb, masked):
        k0 = pl.multiple_of(kb * tq, tq)
        ks = kv_ref[pl.ds(k0, tq), 0:128]
        vs = kv_ref[pl.ds(k0, tq), 128:256]
        kpos = (k0 + lax.broadcasted_iota(jnp.int32, (1, tq), 1)).astype(F32)
        if masked:
            keep = (lax.broadcasted_iota(jnp.int32, (tq, tq), 0)
                    >= lax.broadcasted_iota(jnp.int32, (tq, tq), 1))
        for mp in range(8):
            g, r = mp // 4, mp % 2
            s = _dot_nt(q_ref[:, mp * 128:(mp + 1) * 128], ks) * scale + _diff_slope(g, r) * kpos
            if masked:
                s = jnp.where(keep, s, NEG_INF)
            m_old = m_ref[mp]
            m_new = jnp.maximum(m_old, jnp.max(s, axis=-1, keepdims=True))
            corr = jnp.exp(m_old - m_new)
            p = jnp.exp(s - m_new)
            l_ref[mp] = corr * l_ref[mp] + jnp.sum(p, axis=-1, keepdims=True)
            acc_ref[mp] = corr * acc_ref[mp] + _dot(p.astype(BF16), vs)
            m_ref[mp] = m_new

    def body(kb, c):
        kblock(kb, False)
        return c

    lax.fori_loop(0, qi, body, 0)
    kblock(qi, True)

    lam = lam_ref[...]
    sub_g = subg_ref[...]
    for g in range(2):
        for r in range(2):
            m0 = g * 4 + r
            m1 = g * 4 + 2 + r
            x = acc_ref[m0] / l_ref[m0] - lam * (acc_ref[m1] / l_ref[m1])
            x = _diff_finish(x, g, sub_g, lam_init)
            o_ref[:, g * 128 + r * 64:g * 128 + r * 64 + 64] = x[:, g * 64:g * 64 + 64]


def _diff_prompt(qexp, rowd_bf, lw, lam_init, nb, t, tq):
    return pl.pallas_call(
        functools.partial(_diff_prompt_kernel, tq=tq, lam_init=lam_init),
        grid=(nb, t // tq),
        in_specs=[
            pl.BlockSpec((None, tq, 1024), lambda b, i: (b, i, 0)),
            pl.BlockSpec((None, t, 256), lambda b, i: (b, 0, 0)),
            _const_spec((1, 128)), _const_spec((1, 128)),
        ],
        out_specs=pl.BlockSpec((None, tq, 256), lambda b, i: (b, i, 0)),
        out_shape=jax.ShapeDtypeStruct((nb, t, 256), F32),
        scratch_shapes=[pltpu.VMEM((8, tq, 1), F32), pltpu.VMEM((8, tq, 1), F32),
                        pltpu.VMEM((8, tq, 128), F32)],
        compiler_params=_params(("parallel", "arbitrary")),
        name="diff_prompt",
    )(qexp.reshape(nb, t, 1024), rowd_bf.reshape(nb, t, 256), lw["lam_v"], lw["subln_g"])


def _mla_prompt_kernel(q_ref, k_ref, wuv_ref, o_ref, m_ref, l_ref, acc_ref, *, tq):
    qi = pl.program_id(1)
    scale = (MLA_NOPE + MLA_ROPE) ** -0.5
    m_ref[...] = jnp.full_like(m_ref, NEG_INF)
    l_ref[...] = jnp.zeros_like(l_ref)
    acc_ref[...] = jnp.zeros_like(acc_ref)

    def kblock(kb, masked):
        k0 = pl.multiple_of(kb * tq, tq)
        ks = k_ref[pl.ds(k0, tq), :]
        vs = k_ref[pl.ds(k0, tq), 0:128]
        if masked:
            keep = (lax.broadcasted_iota(jnp.int32, (tq, tq), 0)
                    >= lax.broadcasted_iota(jnp.int32, (tq, tq), 1))
        for h in range(MLA_HEADS):
            s = _dot_nt(q_ref[:, h * MLA_PAD:(h + 1) * MLA_PAD], ks) * scale
            if masked:
                s = jnp.where(keep, s, NEG_INF)
            m_old = m_ref[h]
            m_new = jnp.maximum(m_old, jnp.max(s, axis=-1, keepdims=True))
            corr = jnp.exp(m_old - m_new)
            p = jnp.exp(s - m_new)
            l_ref[h] = corr * l_ref[h] + jnp.sum(p, axis=-1, keepdims=True)
            acc_ref[h] = corr * acc_ref[h] + _dot(p.astype(BF16), vs)
            m_ref[h] = m_new

    def body(kb, c):
        kblock(kb, False)
        return c

    lax.fori_loop(0, qi, body, 0)
    kblock(qi, True)

    out = jnp.zeros((tq, BRANCH), F32)
    for h in range(MLA_HEADS):
        o_lat = (acc_ref[h] / l_ref[h]).astype(BF16)
        out = out + _dot(o_lat, wuv_ref[h])
    o_ref[...] = out


def _mla_prompt(qcat, kcat, lw, nb, t, tq):
    return pl.pallas_call(
        functools.partial(_mla_prompt_kernel, tq=tq),
        grid=(nb, t // tq),
        in_specs=[
            pl.BlockSpec((None, tq, 1024), lambda b, i: (b, i, 0)),
            pl.BlockSpec((None, t, MLA_PAD), lambda b, i: (b, 0, 0)),
            _const_spec((MLA_HEADS, 128, BRANCH)),
        ],
        out_specs=pl.BlockSpec((None, tq, 256), lambda b, i: (b, i, 0)),
        out_shape=jax.ShapeDtypeStruct((nb, t, 256), F32),
        scratch_shapes=[pltpu.VMEM((MLA_HEADS, tq, 1), F32), pltpu.VMEM((MLA_HEADS, tq, 1), F32),
                        pltpu.VMEM((MLA_HEADS, tq, 128), F32)],
        compiler_params=_params(("parallel", "arbitrary")),
        name="mla_prompt",
    )(qcat.reshape(nb, t, 1024), kcat.reshape(nb, t, MLA_PAD), lw["w_uv_pad"])


def _conv_prompt_kernel(gb_ref, e_ref, w_ref, o_ref, nb_ref, scr_ref, *, t):
    e = e_ref[...]
    scr_ref[0:8, :] = jnp.zeros((8, BRANCH), F32)
    scr_ref[8:t + 8, :] = e
    w = w_ref[...]
    z = w[0:1, :] * scr_ref[6:t + 6, :] + w[1:2, :] * scr_ref[7:t + 7, :] + w[2:3, :] * e
    o_ref[...] = gb_ref[...] * z
    nb_ref[...] = scr_ref[t + 6:t + 8, :]


def _conv_prompt(gb, e, conv_w, l, nb, t):
    return pl.pallas_call(
        functools.partial(_conv_prompt_kernel, t=t),
        grid=(nb,),
        in_specs=[
            pl.BlockSpec((None, t, 256), lambda b: (b, 0, 0)),
            pl.BlockSpec((None, t, 256), lambda b: (b, 0, 0)),
            pl.BlockSpec((None, CONV_W, 256), lambda b: (l, 0, 0)),
        ],
        out_specs=[pl.BlockSpec((None, t, 256), lambda b: (b, 0, 0)),
                   pl.BlockSpec((None, CONV_W - 1, 256), lambda b: (b, 0, 0))],
        out_shape=[jax.ShapeDtypeStruct((nb, t, 256), F32),
                   jax.ShapeDtypeStruct((nb, CONV_W - 1, 256), F32)],
        scratch_shapes=[pltpu.VMEM((t + 8, 256), F32)],
        compiler_params=_params(("parallel",)),
        name="conv_prompt",
    )(gb.reshape(nb, t, 256), e.reshape(nb, t, 256), conv_w)


def _conv_sample_kernel(gb_ref, e_ref, st_ref, w_ref, o_ref, ns_ref):
    e = e_ref[...]
    w = w_ref[...]
    b0 = st_ref[:, 0:256]
    b1 = st_ref[:, 256:512]
    o_ref[...] = gb_ref[...] * (w[0:1, :] * b0 + w[1:2, :] * b1 + w[2:3, :] * e)
    ns_ref[:, 0:256] = b1
    ns_ref[:, 256:512] = e


def _conv_sample(gb, e, state2, conv_w, l):
    n = gb.shape[0]
    return pl.pallas_call(
        _conv_sample_kernel,
        grid=(1,),
        in_specs=[_const_spec((n, 256)), _const_spec((n, 256)), _const_spec((n, 512)),
                  pl.BlockSpec((None, CONV_W, 256), lambda i: (l, 0, 0))],
        out_specs=[_const_spec((n, 256)), _const_spec((n, 512))],
        out_shape=[jax.ShapeDtypeStruct((n, 256), F32), jax.ShapeDtypeStruct((n, 512), F32)],
        compiler_params=_params(("arbitrary",)),
        name="conv_sample",
    )(gb, e, state2, conv_w)


def _page_copy(cache_ref, buf_ref, sem_ref, l, page, slot, j, page_size):
    return pltpu.make_async_copy(cache_ref.at[l, page],
                                 buf_ref.at[slot, pl.ds(j * page_size, page_size), :],
                                 sem_ref.at[slot])


def _gather_pages(pt_ref, cache_ref, buf_ref, sem_ref, l, n_pages, page_size):
    b = pl.program_id(0)
    nb = pl.num_programs(0)
    slot = b % 2

    def issue(seq, sl):
        for j in range(n_pages):
            _page_copy(cache_ref, buf_ref, sem_ref, l, pt_ref[seq, j], sl, j, page_size).start()

    @pl.when(b == 0)
    def _():
        issue(0, 0)

    @pl.when(b + 1 < nb)
    def _():
        issue(b + 1, 1 - slot)

    for j in range(n_pages):
        _page_copy(cache_ref, buf_ref, sem_ref, l, 0, slot, j, page_size).wait()
    return slot


def _diff_decode_kernel(pt_ref, q_ref, new_ref, bias_ref, lam_ref, subg_ref, cache_ref, o_ref,
                        buf_ref, sem_ref, *, l, n_pages, page_size, chunk, lam_init):
    slot = _gather_pages(pt_ref, cache_ref, buf_ref, sem_ref, l, n_pages, page_size)
    past = n_pages * page_size
    scale = DIFF_D ** -0.5
    qb = q_ref[...].astype(BF16)
    new = new_ref[...]
    k_new = new[:, 0:128].astype(BF16).astype(F32)
    v_new = new[:, 128:256].astype(BF16).astype(F32)
    s_new = jnp.sum(qb.astype(F32) * k_new, axis=-1, keepdims=True) * scale

    n_ch = past // chunk
    s_parts = []
    for c in range(n_ch):
        kc = buf_ref[slot, c * chunk:(c + 1) * chunk, 0:128].astype(BF16)
        s_parts.append(_dot_nt(qb, kc) * scale + bias_ref[:, c * chunk:(c + 1) * chunk])
    m = s_new
    for s in s_parts:
        m = jnp.maximum(m, jnp.max(s, axis=-1, keepdims=True))
    p_new = jnp.exp(s_new - m)
    lsum = p_new
    pv = p_new.astype(BF16).astype(F32) * v_new
    for c in range(n_ch):
        p = jnp.exp(s_parts[c] - m)
        lsum = lsum + jnp.sum(p, axis=-1, keepdims=True)
        vc = buf_ref[slot, c * chunk:(c + 1) * chunk, 128:256].astype(BF16)
        pv = pv + _dot(p.astype(BF16), vc)
    pvn = pv / lsum
    lam = lam_ref[...]
    sub_g = subg_ref[...]
    for g in range(2):
        x = pvn[4 * g:4 * g + 2, :] - lam * pvn[4 * g + 2:4 * g + 4, :]
        x = _diff_finish(x, g, sub_g, lam_init)
        o_ref[2 * g:2 * g + 2, :] = x[:, 64 * g:64 * g + 64]


def _diff_decode(page_table, qexp, rowd, cache, lw, l, lam_init, bias):
    n, n_pages = page_table.shape
    page_size = cache.shape[2]
    past = n_pages * page_size
    kern = functools.partial(_diff_decode_kernel, l=l, n_pages=n_pages, page_size=page_size,
                             chunk=1024, lam_init=lam_init)
    out = pl.pallas_call(
        kern,
        grid_spec=pltpu.PrefetchScalarGridSpec(
            num_scalar_prefetch=1,
            grid=(n,),
            in_specs=[
                pl.BlockSpec((None, 8, 128), lambda b, pt: (b, 0, 0)),
                pl.BlockSpec((None, 1, 256), lambda b, pt: (b, 0, 0)),
                pl.BlockSpec((8, past), lambda b, pt: (0, 0)),
                pl.BlockSpec((1, 128), lambda b, pt: (0, 0)),
                pl.BlockSpec((1, 128), lambda b, pt: (0, 0)),
                pl.BlockSpec(memory_space=pl.ANY),
            ],
            out_specs=pl.BlockSpec((None, 4, 64), lambda b, pt: (b, 0, 0)),
            scratch_shapes=[pltpu.VMEM((2, past, DIFF_ROW), F32), pltpu.SemaphoreType.DMA((2,))],
        ),
        out_shape=jax.ShapeDtypeStruct((n, 4, 64), F32),
        compiler_params=_params(("arbitrary",)),
        name="diff_decode",
    )(page_table, qexp.reshape(n, 8, 128), rowd.reshape(n, 1, 256), bias, lw["lam_v"], lw["subln_g"],
      cache)
    return out.reshape(n, BRANCH)


def _mla_decode_kernel(pt_ref, q_ref, new_ref, wuv_ref, cache_ref, o_ref, buf_ref, sem_ref,
                       *, l, n_pages, page_size, chunk):
    slot = _gather_pages(pt_ref, cache_ref, buf_ref, sem_ref, l, n_pages, page_size)
    past = n_pages * page_size
    scale = (MLA_NOPE + MLA_ROPE) ** -0.5
    q = q_ref[...]
    q_lat = q[:, 0:128]
    q_rope = q[:, 128:128 + MLA_ROPE]
    new = new_ref[...].astype(F32)
    s_new = jnp.sum(q.astype(F32) * new, axis=-1, keepdims=True) * scale
    v_new = new[:, 0:128]

    n_ch = past // chunk
    s_parts = []
    for c in range(n_ch):
        kl = buf_ref[slot, c * chunk:(c + 1) * chunk, 0:128].astype(BF16)
        kr = buf_ref[slot, c * chunk:(c + 1) * chunk, 128:MLA_ROW].astype(BF16)
        s_parts.append((_dot_nt(q_lat, kl) + _dot_nt(q_rope, kr)) * scale)
    m = s_new
    for s in s_parts:
        m = jnp.maximum(m, jnp.max(s, axis=-1, keepdims=True))
    p_new = jnp.exp(s_new - m)
    lsum = p_new
    pv = p_new.astype(BF16).astype(F32) * v_new
    for c in range(n_ch):
        p = jnp.exp(s_parts[c] - m)
        lsum = lsum + jnp.sum(p, axis=-1, keepdims=True)
        vc = buf_ref[slot, c * chunk:(c + 1) * chunk, 0:128].astype(BF16)
        pv = pv + _dot(p.astype(BF16), vc)
    o_lat = (pv / lsum).astype(BF16)
    row = lax.broadcasted_iota(jnp.int32, (8, BRANCH), 0)
    out = jnp.zeros((8, BRANCH), F32)
    for h in range(MLA_HEADS):
        out = out + jnp.where(row == h, _dot(o_lat, wuv_ref[h]), 0.0)
    o_ref[...] = jnp.sum(out, axis=0, keepdims=True)


def _mla_decode(page_table, qcat, kcat, cache, lw, l):
    n, n_pages = page_table.shape
    page_size = cache.shape[2]
    past = n_pages * page_size
    kern = functools.partial(_mla_decode_kernel, l=l, n_pages=n_pages, page_size=page_size, chunk=1024)
    out = pl.pallas_call(
        kern,
        grid_spec=pltpu.PrefetchScalarGridSpec(
            num_scalar_prefetch=1,
            grid=(n,),
            in_specs=[
                pl.BlockSpec((None, 8, MLA_PAD), lambda b, pt: (b, 0, 0)),
                pl.BlockSpec((None, 1, MLA_PAD), lambda b, pt: (b, 0, 0)),
                pl.BlockSpec((MLA_HEADS, 128, BRANCH), lambda b, pt: (0, 0, 0)),
                pl.BlockSpec(memory_space=pl.ANY),
            ],
            out_specs=pl.BlockSpec((None, 1, BRANCH), lambda b, pt: (b, 0, 0)),
            scratch_shapes=[pltpu.VMEM((2, past, MLA_ROW), F32), pltpu.SemaphoreType.DMA((2,))],
        ),
        out_shape=jax.ShapeDtypeStruct((n, 1, BRANCH), F32),
        compiler_params=_params(("arbitrary",)),
        name="mla_decode",
    )(page_table, jnp.pad(qcat.reshape(n, MLA_HEADS, MLA_PAD), ((0, 0), (0, 8 - MLA_HEADS), (0, 0))),
      kcat.reshape(n, 1, MLA_PAD), lw["w_uv_pad"], cache)
    return out.reshape(n, BRANCH)


def _merge_kernel(x_ref, o1_ref, o2_ref, o3_ref, o4_ref, wg_ref, bg_ref, wb_ref, wo_ref, g_ref, b_ref,
                  out_ref, *, alpha):
    x = x_ref[...]
    xb = x.astype(BF16)
    acc = jnp.zeros(x.shape, F32)
    for n, o_ref in enumerate((o1_ref, o2_ref, o3_ref, o4_ref)):
        gate = jax.nn.sigmoid(_dot(xb, wg_ref[:, n * D_MODEL:(n + 1) * D_MODEL]) + bg_ref[n:n + 1, :])
        acc = acc + gate * _dot(o_ref[...].astype(BF16), wb_ref[n])
    mix = _dot(acc.astype(BF16), wo_ref[...])
    out_ref[...] = _layernorm(alpha * x + mix, g_ref[...], b_ref[...])


def _merge(x, branches, lw, w_branch_bf, w_out_bf, b_gate, ln_g, ln_b, l, alpha, tm):
    m = x.shape[0]
    row = lambda i: (i, 0)
    return pl.pallas_call(
        functools.partial(_merge_kernel, alpha=alpha),
        grid=(m // tm,),
        in_specs=[
            pl.BlockSpec((tm, D_MODEL), row),
            pl.BlockSpec((tm, 256), row), pl.BlockSpec((tm, 256), row),
            pl.BlockSpec((tm, 256), row), pl.BlockSpec((tm, 256), row),
            _const_spec((D_MODEL, N_BRANCH * D_MODEL)),
            pl.BlockSpec((None, N_BRANCH, D_MODEL), lambda i: (l, 0, 0)),
            pl.BlockSpec((None, N_BRANCH, BRANCH, D_MODEL), lambda i: (l, 0, 0, 0)),
            pl.BlockSpec((None, D_MODEL, D_MODEL), lambda i: (l, 0, 0)),
            pl.BlockSpec((None, None, 1, D_MODEL), lambda i: (l, 1, 0, 0)),
            pl.BlockSpec((None, None, 1, D_MODEL), lambda i: (l, 1, 0, 0)),
        ],
        out_specs=pl.BlockSpec((tm, D_MODEL), row),
        out_shape=jax.ShapeDtypeStruct((m, D_MODEL), F32),
        compiler_params=_params(("parallel",)),
        name="merge",
    )(x, *branches, lw["w_gate"], b_gate, w_branch_bf, w_out_bf, ln_g, ln_b)


def _place_q():
    pm = np.zeros((256, 1024), np.float32)
    for g in range(2):
        for r in range(2):
            for c in range(2):
                mp = g * 4 + c * 2 + r
                for d in range(DIFF_D):
                    pm[g * 128 + r * 64 + c * 32 + d, mp * 128 + g * 64 + c * 32 + d] = 1.0
    return pm


def _place_r():
    pm = np.zeros((128, 1024), np.float32)
    for h in range(MLA_HEADS):
        for d in range(MLA_ROPE):
            pm[h * 32 + d, h * MLA_PAD + 128 + d] = 1.0
    return pm


def _swap_halves(w, width):
    half = width // 2
    shp = w.shape
    w2 = w.reshape(shp[:-1] + (shp[-1] // width, 2, half))
    return w2[..., ::-1, :].reshape(shp)


def _layer_weights(l, p):
    w_in = p["w_in"][l]
    zeros = lambda n: jnp.zeros((D_MODEL, n), F32)
    k_r = w_in[:, 1152:1184]
    w_main = jnp.concatenate([
        w_in[:, 0:256], w_in[:, 256:512], w_in[:, 512:768], w_in[:, 768:1024], w_in[:, 1024:1152],
        k_r, zeros(96), _swap_halves(k_r, MLA_ROPE), zeros(96),
        w_in[:, 1184:1440], w_in[:, 1440:1696], w_in[:, 1696:1952]], axis=1).astype(BF16)
    w_gate = w_in[:, 1952:].astype(BF16)

    wq = p["mla_w_uq"][l].reshape(MLA_Q_RANK, MLA_HEADS, MLA_NOPE + MLA_ROPE)
    wq_nope = wq[:, :, :MLA_NOPE].reshape(MLA_Q_RANK, 256)
    wq_rope = wq[:, :, MLA_NOPE:].reshape(MLA_Q_RANK, 128)
    w_uq = jnp.concatenate([wq_nope, wq_rope, _swap_halves(wq_rope, MLA_ROPE)], axis=1).astype(BF16)
    wuk = p["mla_w_uk"][l]
    eye_h = jnp.eye(MLA_HEADS, dtype=F32)
    wuk_bd = jnp.einsum("lhn,hk->hnkl", wuk, eye_h)
    wuk_bd = jnp.pad(wuk_bd, ((0, 0), (0, 0), (0, 0), (0, MLA_PAD - MLA_KV_RANK)))
    w_uk = wuk_bd.reshape(256, MLA_HEADS * MLA_PAD).astype(BF16)
    wuv = p["mla_w_uv"][l]
    w_uv_pad = jnp.einsum("lhv,hk->hlkv", wuv, eye_h).reshape(MLA_HEADS, MLA_KV_RANK, BRANCH).astype(BF16)

    lr, li = p["s5_lam_re"][l], p["s5_lam_im"][l]
    step = jnp.exp(p["s5_log_dt"][l])[:, None]
    mag = jnp.exp(lr * step)
    ar = mag * jnp.cos(li * step)
    ai = mag * jnp.sin(li * step)
    den = lr * lr + li * li
    nr, ni = ar - 1.0, ai
    kr = (nr * lr + ni * li) / den
    ki = (ni * lr - nr * li) / den
    b_re, b_im = p["s5_b_re"][l], p["s5_b_im"][l]
    bb_re = kr[..., None] * b_re - ki[..., None] * b_im
    bb_im = kr[..., None] * b_im + ki[..., None] * b_re
    eye_g = jnp.eye(S5_G, dtype=F32)
    expand_b = lambda bb: jnp.einsum("gpc,gh->gchp", bb, eye_g).reshape(256, S5_N)
    s5_bb = jnp.concatenate([expand_b(bb_re), expand_b(bb_im)], axis=1).astype(BF16)
    expand_c = lambda cc: jnp.einsum("gcp,gh->gphc", cc, eye_g).reshape(S5_N, 256)
    s5_cc = jnp.concatenate([expand_c(p["s5_c_re"][l]), expand_c(-p["s5_c_im"][l])], axis=0).astype(BF16)

    lam_p = p["diff_lambda"][l]
    lam_init = 0.8 - 0.6 * math.exp(-0.3 * l)
    lam = jnp.exp(jnp.sum(lam_p[0] * lam_p[1])) - jnp.exp(jnp.sum(lam_p[2] * lam_p[3])) + lam_init
    return dict(
        w_main=w_main, w_gate=w_gate, place_q=jnp.asarray(_place_q(), BF16),
        place_r=jnp.asarray(_place_r(), BF16), w_uq=w_uq, w_uk=w_uk, w_uv_pad=w_uv_pad,
        g_q=p["mla_q_norm_g"][l].reshape(1, 256), g_kv=p["mla_kv_norm_g"][l].reshape(1, 128),
        s5_ar=ar.reshape(1, S5_N), s5_ai=ai.reshape(1, S5_N), s5_bb=s5_bb, s5_cc=s5_cc,
        s5_d=p["s5_d"][l].reshape(1, 256), s5_gw=p["s5_glu_w"][l].astype(BF16),
        s5_gb=p["s5_glu_b"][l].reshape(1, 256),
        lam_v=jnp.full((1, 128), lam, F32), subln_g=jnp.tile(p["diff_subln_g"][l], 2).reshape(1, 128),
    ), lam_init


def _rope_tables(pos):
    half = MLA_ROPE // 2
    inv = ROPE_BASE ** (-jnp.arange(half, dtype=F32) / half)
    ang = pos.astype(F32)[:, None] * inv[None, :]
    cos, sin = jnp.cos(ang), jnp.sin(ang)
    cos_t = jnp.tile(jnp.concatenate([cos, cos], axis=1), (1, LANES // MLA_ROPE))
    sin_t = jnp.tile(jnp.concatenate([-sin, sin], axis=1), (1, LANES // MLA_ROPE))
    return cos_t, sin_t


def kernel(x_prompt, x_sample, cache_diff_kv, cache_mla, state_s5_re, state_s5_im, state_conv, page_table,
           w_in, b_gate, s5_lam_re, s5_lam_im, s5_log_dt, s5_b_re, s5_b_im, s5_c_re, s5_c_im, s5_d,
           s5_glu_w, s5_glu_b, diff_lambda, diff_subln_g, mla_q_norm_g, mla_kv_norm_g, mla_w_uq, mla_w_uk,
           mla_w_uv, conv_w, w_branch, w_out, ffn_w_in, ffn_w_out, ln_g, ln_b):
    p = dict(w_in=w_in, s5_lam_re=s5_lam_re, s5_lam_im=s5_lam_im, s5_log_dt=s5_log_dt,
             s5_b_re=s5_b_re, s5_b_im=s5_b_im, s5_c_re=s5_c_re, s5_c_im=s5_c_im, s5_d=s5_d,
             s5_glu_w=s5_glu_w, s5_glu_b=s5_glu_b, diff_lambda=diff_lambda, diff_subln_g=diff_subln_g,
             mla_q_norm_g=mla_q_norm_g, mla_kv_norm_g=mla_kv_norm_g, mla_w_uq=mla_w_uq, mla_w_uk=mla_w_uk,
             mla_w_uv=mla_w_uv)
    depth = w_in.shape[0]
    nbp, t, _ = x_prompt.shape
    nbs, ts, _ = x_sample.shape
    assert ts == 1
    n_pages = page_table.shape[1]
    page_size = cache_diff_kv.shape[2]
    past_len = n_pages * page_size
    alpha = float((2.0 * depth) ** 0.25)

    ffn_w_in_bf = ffn_w_in.astype(BF16)
    ffn_w_out_bf = ffn_w_out.astype(BF16)
    w_branch_bf = w_branch.astype(BF16)
    w_out_bf = w_out.astype(BF16)
    ln_g4 = ln_g.reshape(depth, 3, 1, D_MODEL)
    ln_b4 = ln_b.reshape(depth, 3, 1, D_MODEL)

    cos_p, sin_p = _rope_tables(jnp.arange(t))
    cos_s, sin_s = _rope_tables(past_len + jnp.arange(ts))
    cos_s = jnp.broadcast_to(cos_s, (nbs, LANES))
    sin_s = jnp.broadcast_to(sin_s, (nbs, LANES))
    kpos = jnp.arange(past_len, dtype=F32)[None, :]
    slopes = jnp.asarray([[_diff_slope(g, r)] for g in range(2) for _c in range(2) for r in range(2)], F32)
    dec_bias = -slopes * (float(past_len) - kpos)

    tm_p, tm_s = 512, nbs
    yp = x_prompt.reshape(nbp * t, D_MODEL)
    ys = x_sample.reshape(nbs * ts, D_MODEL)
    zeros_state = jnp.zeros((nbp, S5_N), F32)
    outs_p, outs_s = [], []
    for l in range(depth):
        lw, lam_init = _layer_weights(l, p)

        yp = _ffn_ln(yp, ffn_w_in_bf, ffn_w_out_bf, ln_g4, ln_b4, l, 0, 0, alpha, tm_p)
        u, qexp, rowd, rowd_bf, qcat, rowm, kcat, gb, e = _in_proj(yp, lw, cos_p, sin_p, tm_p)
        u_tm = u.reshape(nbp, t, 256).transpose(1, 0, 2).reshape(t * nbp, 256)
        o_s5_tm, sre, sim = _s5(u_tm, zeros_state, zeros_state, lw, nbp, 128)
        o_s5 = o_s5_tm.reshape(t, nbp, 256).transpose(1, 0, 2).reshape(nbp * t, 256)
        o_d = _diff_prompt(qexp, rowd_bf, lw, lam_init, nbp, t, 256).reshape(nbp * t, 256)
        o_m = _mla_prompt(qcat, kcat, lw, nbp, t, 256).reshape(nbp * t, 256)
        o_c, nbuf = _conv_prompt(gb, e, conv_w, l, nbp, t)
        o_c = o_c.reshape(nbp * t, 256)
        yp = _merge(yp, (o_s5, o_d, o_m, o_c), lw, w_branch_bf, w_out_bf, b_gate, ln_g4, ln_b4, l, alpha, 256)
        yp = _ffn_ln(yp, ffn_w_in_bf, ffn_w_out_bf, ln_g4, ln_b4, l, 1, 2, alpha, tm_p)
        outs_p.append((rowd.reshape(nbp, t, DIFF_ROW), rowm.reshape(nbp, t, MLA_ROW),
                       sre.reshape(nbp, S5_G, S5_P), sim.reshape(nbp, S5_G, S5_P), nbuf))

        ys = _ffn_ln(ys, ffn_w_in_bf, ffn_w_out_bf, ln_g4, ln_b4, l, 0, 0, alpha, tm_s)
        u, qexp, rowd, rowd_bf, qcat, rowm, kcat, gb, e = _in_proj(ys, lw, cos_s, sin_s, tm_s)
        o_s5, sre, sim = _s5(u, state_s5_re[l].reshape(nbs, S5_N), state_s5_im[l].reshape(nbs, S5_N),
                             lw, nbs, 1)
        o_d = _diff_decode(page_table, qexp, rowd, cache_diff_kv, lw, l, lam_init, dec_bias)
        o_m = _mla_decode(page_table, qcat, kcat, cache_mla, lw, l)
        o_c, nstate = _conv_sample(gb, e, state_conv[l].reshape(nbs, 2 * 256), conv_w, l)
        ys = _merge(ys, (o_s5, o_d, o_m, o_c), lw, w_branch_bf, w_out_bf, b_gate, ln_g4, ln_b4, l, alpha, tm_s)
        ys = _ffn_ln(ys, ffn_w_in_bf, ffn_w_out_bf, ln_g4, ln_b4, l, 1, 2, alpha, tm_s)
        outs_s.append((rowd.reshape(nbs, ts, DIFF_ROW), rowm.reshape(nbs, ts, MLA_ROW),
                       sre.reshape(nbs, S5_G, S5_P), sim.reshape(nbs, S5_G, S5_P),
                       nstate.reshape(nbs, CONV_W - 1, 256)))

    stk = lambda outs, i: jnp.stack([o[i] for o in outs], axis=0)
    return (yp.reshape(nbp, t, D_MODEL), ys.reshape(nbs, ts, D_MODEL),
            stk(outs_p, 0), stk(outs_p, 1), stk(outs_p, 2), stk(outs_p, 3), stk(outs_p, 4),
            stk(outs_s, 0), stk(outs_s, 1), stk(outs_s, 2), stk(outs_s, 3), stk(outs_s, 4))
```
